```python
import math
import jax, jax.numpy as jnp
from jax import lax
import numpy as np

D_MODEL = 4096
BATCH = 8
SEQ = 2048
DEPTH = 1
DEC_BATCH = 2
DEC_SEQ = 4096
PAST_LEN = 128

SSM_WIDTH = 2048
SSM_GROUP = 16
SSM_GROUPS = SSM_WIDTH // SSM_GROUP
SSM_STATE = 64
N_DIR = 2
DT_MIN = 1e-3
DT_MAX = 1e-1
LAMBDA_RE_MAX = -1e-4
FFT_WIDTH = 2048
FFT_GROUPS = 4
FFT_GROUP = FFT_WIDTH // FFT_GROUPS
N_BRANCH = 2
IN_WIDTH = 2 * SSM_WIDTH + 2 * FFT_WIDTH + N_BRANCH * D_MODEL
RMS_EPS = 1e-6

kernel_name = "hybrid_s5_fnet_gated_encoder"


def _rmsnorm(x, g):
    xf = x.astype(jnp.float32)
    y = xf * lax.rsqrt(jnp.mean(xf * xf, axis=-1, keepdims=True) + RMS_EPS)
    return (y * g.astype(jnp.float32)).astype(x.dtype)


def _ssm_combine(e1, e2):
    a1, b1 = e1
    a2, b2 = e2
    return a1 * a2, a2 * b1 + b2


def _s5_direction(u, lam_re, lam_im, log_dt, b_re, b_im, c_re, c_im, reverse):
    f32 = jnp.float32
    lam = lax.complex(jnp.minimum(lam_re.astype(f32), LAMBDA_RE_MAX), lam_im.astype(f32))
    dt = jnp.exp(log_dt.astype(f32))[:, None]
    lam_bar = jnp.exp(lam * dt)
    b_bar = ((lam_bar - 1.0) / lam)[..., None] * lax.complex(b_re.astype(f32), b_im.astype(f32))
    c_mat = lax.complex(c_re.astype(f32), c_im.astype(f32))
    bu = jnp.einsum('blgc,gpc->blgp', u.astype(jnp.complex64), b_bar)
    a = jnp.broadcast_to(lam_bar, bu.shape)
    _, states = lax.associative_scan(_ssm_combine, (a, bu), reverse=reverse, axis=1)
    return jnp.einsum('blgp,gcp->blgc', states, c_mat).real


def _s5_bidirectional(u, lam_re, lam_im, log_dt, b_re, b_im, c_re, c_im, d_skip):
    bsz, seq, _ = u.shape
    ug = u.astype(jnp.float32).reshape(bsz, seq, SSM_GROUPS, SSM_GROUP)
    y_fwd = _s5_direction(ug, lam_re[0], lam_im[0], log_dt[0], b_re[0], b_im[0], c_re[0], c_im[0], False)
    y_bwd = _s5_direction(ug, lam_re[1], lam_im[1], log_dt[1], b_re[1], b_im[1], c_re[1], c_im[1], True)
    y = y_fwd + y_bwd + d_skip.astype(jnp.float32).reshape(SSM_GROUPS, SSM_GROUP) * ug
    return y.reshape(bsz, seq, SSM_WIDTH)


def _fourier_mix(u, w_fft):
    bsz, seq, _ = u.shape
    ug = u.astype(jnp.float32).reshape(bsz, seq, FFT_GROUPS, FFT_GROUP)
    f = jnp.fft.fft2(ug, axes=(1, 3), norm="ortho").real
    y = jnp.einsum('blgc,gcd->blgd', f, w_fft.astype(jnp.float32))
    return y.reshape(bsz, seq, FFT_WIDTH)


def _layer(x, pre_g, post_g, w_in, lam_re, lam_im, log_dt, b_re, b_im, c_re, c_im, d_skip,
           w_glu, w_fft, w_up_ssm, w_up_fft, w_out):
    dt = x.dtype
    h = _rmsnorm(x, pre_g)
    proj = h @ w_in
    s1 = SSM_WIDTH
    s2 = 2 * SSM_WIDTH
    s3 = s2 + FFT_WIDTH
    s4 = s3 + FFT_WIDTH
    u_ssm, z_ssm, u_fft, z_fft, gate_logits = jnp.split(proj, [s1, s2, s3, s4], axis=-1)
    y_ssm = jax.nn.gelu(_s5_bidirectional(u_ssm, lam_re, lam_im, log_dt, b_re, b_im, c_re, c_im, d_skip)).astype(dt)
    glu_val, glu_gate = jnp.split(y_ssm @ w_glu, 2, axis=-1)
    y_ssm = glu_val * jax.nn.sigmoid(glu_gate) * jax.nn.silu(z_ssm)
    y_fft = _fourier_mix(u_fft, w_fft).astype(dt) * jax.nn.silu(z_fft)
    p_ssm = y_ssm @ w_up_ssm
    p_fft = y_fft @ w_up_fft
    g = jax.nn.sigmoid(gate_logits).reshape(*gate_logits.shape[:-1], N_BRANCH, D_MODEL)
    merged = g[..., 0, :] * p_ssm + g[..., 1, :] * p_fft
    out = merged @ w_out
    return x + _rmsnorm(out, post_g)


def _trunk(x, pre_norm, post_norm, w_in, lambda_re, lambda_im, log_dt, b_re, b_im, c_re, c_im,
           d_skip, w_glu, w_fft, w_up_ssm, w_up_fft, w_out):
    for l in range(DEPTH):
        x = _layer(x, pre_norm[l], post_norm[l], w_in[l], lambda_re[l], lambda_im[l], log_dt[l],
                   b_re[l], b_im[l], c_re[l], c_im[l], d_skip[l], w_glu[l], w_fft[l],
                   w_up_ssm[l], w_up_fft[l], w_out[l])
    return x


def setup_inputs(seed: int = 0) -> dict:
    key = jax.random.key(seed)
    ks = jax.random.split(key, 20)
    f32 = jnp.float32
    G, P, C = SSM_GROUPS, SSM_STATE, SSM_GROUP
    nrm = lambda k, shape, scale: jax.random.normal(k, shape, f32) * scale
    x_prompt = jax.random.normal(ks[0], (BATCH, SEQ, D_MODEL), f32)
    x_sample = jax.random.normal(ks[1], (DEC_BATCH, DEC_SEQ, D_MODEL), f32)
    pre_norm = 1.0 + nrm(ks[2], (DEPTH, D_MODEL), 0.05)
    post_norm = 1.0 + nrm(ks[3], (DEPTH, D_MODEL), 0.05)
    w_in = nrm(ks[4], (DEPTH, D_MODEL, IN_WIDTH), D_MODEL ** -0.5)
    lambda_re = -0.5 + nrm(ks[5], (DEPTH, N_DIR, G, P), 0.01)
    lambda_im = jnp.pi * jnp.arange(P, dtype=f32) + nrm(ks[6], (DEPTH, N_DIR, G, P), 0.01)
    log_dt = jax.random.uniform(ks[7], (DEPTH, N_DIR, G), f32, math.log(DT_MIN), math.log(DT_MAX))
    b_re = nrm(ks[8], (DEPTH, N_DIR, G, P, C), (2.0 * C) ** -0.5)
    b_im = nrm(ks[9], (DEPTH, N_DIR, G, P, C), (2.0 * C) ** -0.5)
    c_re = nrm(ks[10], (DEPTH, N_DIR, G, C, P), (2.0 * P) ** -0.5)
    c_im = nrm(ks[11], (DEPTH, N_DIR, G, C, P), (2.0 * P) ** -0.5)
    d_skip = nrm(ks[12], (DEPTH, SSM_WIDTH), 1.0)
    w_glu = nrm(ks[13], (DEPTH, SSM_WIDTH, 2 * SSM_WIDTH), SSM_WIDTH ** -0.5)
    w_fft = nrm(ks[14], (DEPTH, FFT_GROUPS, FFT_GROUP, FFT_GROUP), FFT_GROUP ** -0.5)
    w_up_ssm = nrm(ks[15], (DEPTH, SSM_WIDTH, D_MODEL), SSM_WIDTH ** -0.5)
    w_up_fft = nrm(ks[16], (DEPTH, FFT_WIDTH, D_MODEL), FFT_WIDTH ** -0.5)
    w_out = nrm(ks[17], (DEPTH, D_MODEL, D_MODEL), D_MODEL ** -0.5)
    return {"x_prompt": x_prompt, "x_sample": x_sample, "pre_norm": pre_norm, "post_norm": post_norm,
            "w_in": w_in, "lambda_re": lambda_re, "lambda_im": lambda_im, "log_dt": log_dt,
            "b_re": b_re, "b_im": b_im, "c_re": c_re, "c_im": c_im, "d_skip": d_skip,
            "w_glu": w_glu, "w_fft": w_fft, "w_up_ssm": w_up_ssm, "w_up_fft": w_up_fft, "w_out": w_out}


def reference(x_prompt, x_sample, pre_norm, post_norm, w_in, lambda_re, lambda_im, log_dt,
              b_re, b_im, c_re, c_im, d_skip, w_glu, w_fft, w_up_ssm, w_up_fft, w_out):
    y_prompt = _trunk(x_prompt, pre_norm, post_norm, w_in, lambda_re, lambda_im, log_dt, b_re, b_im,
                      c_re, c_im, d_skip, w_glu, w_fft, w_up_ssm, w_up_fft, w_out)
    y_sample = _trunk(x_sample, pre_norm, post_norm, w_in, lambda_re, lambda_im, log_dt, b_re, b_im,
                      c_re, c_im, d_skip, w_glu, w_fft, w_up_ssm, w_up_fft, w_out)
    return (y_prompt, y_sample)
```

```python
import functools
import math

import numpy as np
import jax
import jax.numpy as jnp
from jax import lax
from jax.experimental import pallas as pl
from jax.experimental.pallas import tpu as pltpu

RMS_EPS = 1e-6
LAMBDA_RE_MAX = -1e-4
SSM_CHUNK = 16
N_SCAN_LANES = 8
VMEM_LIMIT_BYTES = 56 * 1024 * 1024
LANE = 128

F32 = jnp.float32
BF16 = jnp.bfloat16


def _params(*sem):
    return pltpu.CompilerParams(dimension_semantics=sem, vmem_limit_bytes=VMEM_LIMIT_BYTES)


def _tile(n, pref, *offsets):
    if n <= pref and all(o % n == 0 for o in offsets):
        return n
    t = (min(pref, n) // LANE) * LANE
    while t >= LANE:
        if n % t == 0 and all(o % t == 0 for o in offsets):
            return t
        t -= LANE
    raise ValueError(f"no tile for {n} {pref} {offsets}")


def _sigmoid(x):
    return 1.0 / (1.0 + jnp.exp(-x))


def _rmsnorm_kernel(x_ref, g_ref, o_ref):
    x = x_ref[...]
    ms = jnp.mean(x * x, axis=-1, keepdims=True)
    o_ref[...] = (x * lax.rsqrt(ms + RMS_EPS) * g_ref[...]).astype(o_ref.dtype)


def _rmsnorm(x, g, out_dtype):
    t, d = x.shape
    tm = _tile(t, 256)
    return pl.pallas_call(
        _rmsnorm_kernel,
        grid=(t // tm,),
        in_specs=[pl.BlockSpec((tm, d), lambda i: (i, 0)), pl.BlockSpec((1, d), lambda i: (0, 0))],
        out_specs=pl.BlockSpec((tm, d), lambda i: (i, 0)),
        out_shape=jax.ShapeDtypeStruct((t, d), out_dtype),
        compiler_params=_params("parallel"),
        name="rmsnorm",
    )(x, g.reshape(1, d))


def _mm_kernel(a_ref, b_ref, o_ref):
    o_ref[...] = jnp.dot(a_ref[...], b_ref[...], preferred_element_type=F32).astype(o_ref.dtype)


def _matmul(a, b, out_dtype, name, tm_pref=1024, tn_pref=1024):
    m, k = a.shape
    _, n = b.shape
    tm, tn = _tile(m, tm_pref), _tile(n, tn_pref)
    return pl.pallas_call(
        _mm_kernel,
        grid=(m // tm, n // tn),
        in_specs=[pl.BlockSpec((tm, k), lambda i, j: (i, 0)), pl.BlockSpec((k, tn), lambda i, j: (0, j))],
        out_specs=pl.BlockSpec((tm, tn), lambda i, j: (i, j)),
        out_shape=jax.ShapeDtypeStruct((m, n), out_dtype),
        compiler_params=_params("parallel", "parallel"),
        name=name,
    )(a, b)


def _proj_t_kernel(w_ref, h_ref, o_ref):
    r = lax.dot_general(w_ref[...], h_ref[...], (((1,), (1,)), ((), ())), preferred_element_type=F32)
    o_ref[...] = r.reshape(o_ref.shape)


def _proj_ssm_t(h, w_t, groups, cpg):
    t, d = h.shape
    sw = w_t.shape[0]
    nc = t // SSM_CHUNK
    tm, tn = _tile(nc, 512), _tile(sw, 1024)
    hv = h.reshape(nc, SSM_CHUNK * d)
    return pl.pallas_call(
        _proj_t_kernel,
        grid=(nc // tm, SSM_CHUNK, sw // tn),
        in_specs=[pl.BlockSpec((tn, d), lambda i, s, j: (j, 0)),
                  pl.BlockSpec((tm, d), lambda i, s, j: (i, s))],
        out_specs=pl.BlockSpec((tn // cpg, 1, cpg, tm), lambda i, s, j: (j, s, 0, i)),
        out_shape=jax.ShapeDtypeStruct((groups, SSM_CHUNK, cpg, nc), F32),
        compiler_params=_params("parallel", "parallel", "parallel"),
        name="proj_ssm_t",
    )(w_t, hv)


def _ssm_kernel(u_ref, mt_ref, bct_ref, cct_ref, prm_ref, o_ref, *, chunks_per_seq, n_state):
    p = n_state
    v = u_ref[0]
    vb = v.astype(BF16)
    nc = v.shape[1]
    z = jnp.dot(bct_ref[0], vb, preferred_element_type=F32)
    fr, fi, br, bi = z[0:p], z[p:2 * p], z[2 * p:3 * p], z[3 * p:4 * p]
    pos = lax.broadcasted_iota(jnp.int32, (1, nc), 1) % chunks_per_seq
    prm = prm_ref[0]
    nsteps = int(math.log2(chunks_per_seq))
    for k in range(nsteps):
        d = 1 << k
        far, fai = prm[0:p, k:k + 1], prm[0:p, N_SCAN_LANES + k:N_SCAN_LANES + k + 1]
        bar = prm[0:p, 2 * N_SCAN_LANES + k:2 * N_SCAN_LANES + k + 1]
        bai = prm[0:p, 3 * N_SCAN_LANES + k:3 * N_SCAN_LANES + k + 1]
        mf = pos >= d
        sr = jnp.where(mf, pltpu.roll(fr, d, 1), 0.0)
        si = jnp.where(mf, pltpu.roll(fi, d, 1), 0.0)
        fr, fi = fr + far * sr - fai * si, fi + far * si + fai * sr
        mb = pos < chunks_per_seq - d
        sr = jnp.where(mb, pltpu.roll(br, nc - d, 1), 0.0)
        si = jnp.where(mb, pltpu.roll(bi, nc - d, 1), 0.0)
        br, bi = br + bar * sr - bai * si, bi + bar * si + bai * sr
    mf = pos >= 1
    mb = pos < chunks_per_seq - 1
    s_in = jnp.concatenate([
        jnp.where(mf, pltpu.roll(fr, 1, 1), 0.0), jnp.where(mf, pltpu.roll(fi, 1, 1), 0.0),
        jnp.where(mb, pltpu.roll(br, nc - 1, 1), 0.0), jnp.where(mb, pltpu.roll(bi, nc - 1, 1), 0.0),
    ], axis=0).astype(BF16)
    d_skip = prm[:, 4 * N_SCAN_LANES:4 * N_SCAN_LANES + 1]
    y = (jnp.dot(mt_ref[0], vb, preferred_element_type=F32)
         + jnp.dot(cct_ref[0], s_in, preferred_element_type=F32)
         + d_skip * v)
    o_ref[0] = jax.nn.gelu(y, approximate=True).astype(o_ref.dtype)


def _ssm(u, mt, bct, cct, prm, chunks_per_seq, n_state):
    g, r, nc = u.shape
    assert chunks_per_seq & (chunks_per_seq - 1) == 0 and chunks_per_seq <= (1 << N_SCAN_LANES)
    assert 4 * n_state == r
    blk3 = lambda a: pl.BlockSpec((1,) + a.shape[1:], lambda i: (i, 0, 0))
    return pl.pallas_call(
        functools.partial(_ssm_kernel, chunks_per_seq=chunks_per_seq, n_state=n_state),
        grid=(g,),
        in_specs=[blk3(u), blk3(mt), blk3(bct), blk3(cct), blk3(prm)],
        out_specs=pl.BlockSpec((1, r, nc), lambda i: (i, 0, 0)),
        out_shape=jax.ShapeDtypeStruct((g, r, nc), BF16),
        compiler_params=_params("parallel"),
        name="ssm_chunked",
    )(u, mt, bct, cct, prm)


def _ssm_params(lambda_re, lambda_im, log_dt, b_re, b_im, c_re, c_im, d_skip):
    hi = lax.Precision.HIGHEST
    ndir, g, p = lambda_re.shape
    cpg = b_re.shape[-1]
    tc = SSM_CHUNK
    lr = jnp.minimum(lambda_re.astype(F32), LAMBDA_RE_MAX)
    li = lambda_im.astype(F32)
    dt = jnp.exp(log_dt.astype(F32))[..., None]
    xr, xi = lr * dt, li * dt

    def cpow(k):
        mag = jnp.exp(k * xr)
        return mag * jnp.cos(k * xi), mag * jnp.sin(k * xi)

    ar, ai = cpow(1.0)
    den = lr * lr + li * li
    qr = ((ar - 1.0) * lr + ai * li) / den
    qi = (ai * lr - (ar - 1.0) * li) / den
    bre, bim = b_re.astype(F32), b_im.astype(F32)
    bbr = qr[..., None] * bre - qi[..., None] * bim
    bbi = qr[..., None] * bim + qi[..., None] * bre
    ks = jnp.arange(tc + 1, dtype=F32)[:, None, None, None]
    pr, pi = cpow(ks)
    cre, cim = c_re.astype(F32)[None], c_im.astype(F32)[None]
    ecr = cre * pr[:, :, :, None, :] - cim * pi[:, :, :, None, :]
    eci = cre * pi[:, :, :, None, :] + cim * pr[:, :, :, None, :]
    kern = (jnp.einsum('kdgcp,dgpe->dgkce', ecr[:tc], bbr, precision=hi)
            - jnp.einsum('kdgcp,dgpe->dgkce', eci[:tc], bbi, precision=hi))
    tt = jnp.arange(tc)
    diff = tt[:, None] - tt[None, :]
    kf = jnp.where((diff >= 0)[None, :, :, None, None], kern[0][:, jnp.clip(diff, 0, tc - 1)], 0.0)
    kb = jnp.where((diff <= 0)[None, :, :, None, None], kern[1][:, jnp.clip(-diff, 0, tc - 1)], 0.0)
    mt = (kf + kb).transpose(0, 1, 3, 2, 4).reshape(g, tc * cpg, tc * cpg)
    prf, pif = pr[tc - 1 - tt, 0], pi[tc - 1 - tt, 0]
    prb, pib = pr[tt, 1], pi[tt, 1]

    def state_rows(wr, wi, br_, bi_):
        re = wr.transpose(1, 2, 0)[..., None] * br_[:, :, None, :] - wi.transpose(1, 2, 0)[..., None] * bi_[:, :, None, :]
        im = wr.transpose(1, 2, 0)[..., None] * bi_[:, :, None, :] + wi.transpose(1, 2, 0)[..., None] * br_[:, :, None, :]
        return re.reshape(g, p, tc * cpg), im.reshape(g, p, tc * cpg)

    zfr, zfi = state_rows(prf, pif, bbr[0], bbi[0])
    zbr, zbi = state_rows(prb, pib, bbr[1], bbi[1])
    bct = jnp.concatenate([zfr, zfi, zbr, zbi], axis=1)
    def out_cols(e):
        return e.transpose(1, 0, 2, 3).reshape(g, tc * cpg, p)

    cct = jnp.concatenate([out_cols(ecr[1 + tt, 0]), -out_cols(eci[1 + tt, 0]),
                           out_cols(ecr[tc - tt, 1]), -out_cols(eci[tc - tt, 1])], axis=2)
    steps = (tc * 2.0 ** jnp.arange(N_SCAN_LANES, dtype=F32))[:, None, None, None]
    sr, si = cpow(steps)
    scan = jnp.concatenate([sr[:, 0], si[:, 0], sr[:, 1], si[:, 1]], axis=0)
    prm = jnp.zeros((g, tc * cpg, LANE), F32)
    prm = prm.at[:, :p, :4 * N_SCAN_LANES].set(scan.transpose(1, 2, 0))
    dsk = jnp.tile(d_skip.astype(F32).reshape(g, 1, cpg), (1, tc, 1)).reshape(g, tc * cpg)
    prm = prm.at[:, :, 4 * N_SCAN_LANES].set(dsk)
    return mt.astype(BF16), bct.astype(BF16), cct.astype(BF16), prm


def _glu_kernel(yt_ref, wv_ref, wg_ref, z_ref, o_ref):
    a = yt_ref[...].reshape(wv_ref.shape[0], yt_ref.shape[-1])
    dn = (((0,), (0,)), ((), ()))
    val = lax.dot_general(a, wv_ref[...], dn, preferred_element_type=F32)
    gate = lax.dot_general(a, wg_ref[...], dn, preferred_element_type=F32)
    z = z_ref[...].astype(F32)
    o_ref[...] = (val * _sigmoid(gate) * (z * _sigmoid(z))).astype(o_ref.dtype)


def _glu(yt, w_glu, rest, z_off):
    g, tc, cpg, nc = yt.shape
    sw = g * cpg
    nr = rest.shape[1]
    tm, tn = _tile(nc, 512), _tile(sw, 1024, z_off, nr)
    nb = sw // tn
    rv = rest.reshape(nc, tc * nr)
    out = pl.pallas_call(
        _glu_kernel,
        grid=(nc // tm, tc, nb),
        in_specs=[pl.BlockSpec((g, 1, cpg, tm), lambda i, s, j: (0, s, 0, i)),
                  pl.BlockSpec((sw, tn), lambda i, s, j: (0, j)),
                  pl.BlockSpec((sw, tn), lambda i, s, j: (0, nb + j)),
                  pl.BlockSpec((tm, tn), lambda i, s, j: (i, s * (nr // tn) + z_off // tn + j))],
        out_specs=pl.BlockSpec((tm, tn), lambda i, s, j: (i, s * nb + j)),
        out_shape=jax.ShapeDtypeStruct((nc, tc * sw), BF16),
        compiler_params=_params("parallel", "parallel", "parallel"),
        name="glu_gate",
    )(yt, w_glu, w_glu, rv)
    return out.reshape(nc * tc, sw)


def _dft_tables(n, scale, split=64):
    split = min(split, n)
    k = np.arange(n, dtype=np.int64)[:, None]
    hi_ang = 2.0 * np.pi * ((k * (np.arange(n // split, dtype=np.int64)[None, :] * split)) % n) / n
    lo_ang = 2.0 * np.pi * ((k * np.arange(split, dtype=np.int64)[None, :]) % n) / n
    c1, s1 = jnp.asarray(np.cos(hi_ang), F32)[:, :, None], jnp.asarray(np.sin(hi_ang), F32)[:, :, None]
    c2, s2 = jnp.asarray(np.cos(lo_ang) * scale, F32)[:, None, :], jnp.asarray(np.sin(lo_ang) * scale, F32)[:, None, :]
    cos = (c1 * c2 - s1 * s2).reshape(n, n).astype(BF16)
    nsin = (-(s1 * c2 + c1 * s2)).reshape(n, n).astype(BF16)
    return cos, nsin


def _chan_dft_kernel(u_ref, cs_ref, xc_ref, xs_ref):
    r = jnp.dot(u_ref[...], cs_ref[...], preferred_element_type=F32)
    w = xc_ref.shape[1]
    xc_ref[...] = r[:, :w].astype(xc_ref.dtype)
    xs_ref[...] = r[:, w:].astype(xs_ref.dtype)


def _chan_dft(rest, u_off, fw, cg):
    t = rest.shape[0]
    k = np.arange(cg, dtype=np.int64)
    ang = 2.0 * np.pi * ((k[:, None] * k[None, :]) % cg) / cg
    cs = jnp.asarray(np.concatenate([np.cos(ang), np.sin(ang)], axis=1) / math.sqrt(cg), F32).astype(BF16)
    tm = _tile(t, 1024)
    assert u_off % cg == 0
    spec = pl.BlockSpec((tm, cg), lambda i, j: (i, j))
    return pl.pallas_call(
        _chan_dft_kernel,
        grid=(t // tm, fw // cg),
        in_specs=[pl.BlockSpec((tm, cg), lambda i, j: (i, u_off // cg + j)),
                  pl.BlockSpec((cg, 2 * cg), lambda i, j: (0, 0))],
        out_specs=[spec, spec],
        out_shape=[jax.ShapeDtypeStruct((t, fw), BF16)] * 2,
        compiler_params=_params("parallel", "parallel"),
        name="chan_dft",
    )(rest, cs)


def _seq_dft_kernel(cos_ref, nsin_ref, xc_ref, xs_ref, w_ref, z_ref, o_ref):
    f = (jnp.dot(cos_ref[...], xc_ref[0], preferred_element_type=F32)
         + jnp.dot(nsin_ref[...], xs_ref[0], preferred_element_type=F32))
    y = jnp.dot(f.astype(BF16), w_ref[0], preferred_element_type=F32)
    z = z_ref[0].astype(F32)
    o_ref[0] = (y * (z * _sigmoid(z))).astype(o_ref.dtype)


def _seq_dft(xc, xs, w_fft, rest, z_off, bsz, seq):
    t, fw = xc.shape
    ng, cg, _ = w_fft.shape
    nr = rest.shape[1]
    cos, nsin = _dft_tables(seq, 1.0 / math.sqrt(seq))
    tk = _tile(seq, 512)
    assert z_off % cg == 0
    out = pl.pallas_call(
        _seq_dft_kernel,
        grid=(bsz, ng, seq // tk),
        in_specs=[pl.BlockSpec((tk, seq), lambda b, j, k: (k, 0)),
                  pl.BlockSpec((tk, seq), lambda b, j, k: (k, 0)),
                  pl.BlockSpec((1, seq, cg), lambda b, j, k: (b, 0, j)),
                  pl.BlockSpec((1, seq, cg), lambda b, j, k: (b, 0, j)),
                  pl.BlockSpec((1, cg, cg), lambda b, j, k: (j, 0, 0)),
                  pl.BlockSpec((1, tk, cg), lambda b, j, k: (b, k, z_off // cg + j))],
        out_specs=pl.BlockSpec((1, tk, cg), lambda b, j, k: (b, k, j)),
        out_shape=jax.ShapeDtypeStruct((bsz, seq, fw), BF16),
        compiler_params=_params("parallel", "parallel", "parallel"),
        name="seq_dft",
    )(cos, nsin, xc.reshape(bsz, seq, fw), xs.reshape(bsz, seq, fw), w_fft, rest.reshape(bsz, seq, nr))
    return out.reshape(t, fw)


def _merge_kernel(ys_ref, yf_ref, ws_ref, wf_ref, g0_ref, g1_ref, o_ref):
    ps = jnp.dot(ys_ref[...], ws_ref[...], preferred_element_type=F32)
    pf = jnp.dot(yf_ref[...], wf_ref[...], preferred_element_type=F32)
    g0 = _sigmoid(g0_ref[...].astype(F32))
    g1 = _sigmoid(g1_ref[...].astype(F32))
    o_ref[...] = (g0 * ps + g1 * pf).astype(o_ref.dtype)


def _merge(ys, yf, w_up_ssm, w_up_fft, rest, g_off):
    t, sw = ys.shape
    fw = yf.shape[1]
    d = w_up_ssm.shape[1]
    tm, tn = _tile(t, 1024), _tile(d, 1024, g_off)
    return pl.pallas_call(
        _merge_kernel,
        grid=(t // tm, d // tn),
        in_specs=[pl.BlockSpec((tm, sw), lambda i, j: (i, 0)),
                  pl.BlockSpec((tm, fw), lambda i, j: (i, 0)),
                  pl.BlockSpec((sw, tn), lambda i, j: (0, j)),
                  pl.BlockSpec((fw, tn), lambda i, j: (0, j)),
                  pl.BlockSpec((tm, tn), lambda i, j: (i, g_off // tn + j)),
                  pl.BlockSpec((tm, tn), lambda i, j: (i, (g_off + d) // tn + j))],
        out_specs=pl.BlockSpec((tm, tn), lambda i, j: (i, j)),
        out_shape=jax.ShapeDtypeStruct((t, d), BF16),
        compiler_params=_params("parallel", "parallel"),
        name="merge",
    )(ys, yf, w_up_ssm, w_up_fft, rest, rest)


def _post_kernel(o_ref, x_ref, g_ref, y_ref):
    o = o_ref[...]
    ms = jnp.mean(o * o, axis=-1, keepdims=True)
    y_ref[...] = x_ref[...] + o * lax.rsqrt(ms + RMS_EPS) * g_ref[...]


def _post(o, x, g):
    t, d = x.shape
    tm = _tile(t, 256)
    spec = pl.BlockSpec((tm, d), lambda i: (i, 0))
    return pl.pallas_call(
        _post_kernel,
        grid=(t // tm,),
        in_specs=[spec, spec, pl.BlockSpec((1, d), lambda i: (0, 0))],
        out_specs=spec,
        out_shape=jax.ShapeDtypeStruct((t, d), F32),
        compiler_params=_params("parallel"),
        name="post_norm",
    )(o, x, g.reshape(1, d))


def _layer(x3, pre_g, post_g, w_in, ssm_ops, d_skip_shape, w_glu, w_fft, w_up_ssm, w_up_fft, w_out, dims):
    bsz, seq, d = x3.shape
    sw, fw, groups, cpg, n_state = dims
    x = x3.reshape(bsz * seq, d)
    w_ssm_t, w_rest = w_in
    h = _rmsnorm(x, pre_g, BF16)
    u_t = _proj_ssm_t(h, w_ssm_t, groups, cpg)
    rest = _matmul(h, w_rest, BF16, "proj_rest")
    z_ssm_off, u_fft_off, z_fft_off, gate_off = 0, sw, sw + fw, sw + 2 * fw
    mt, bct, cct, prm = ssm_ops
    nc = bsz * seq // SSM_CHUNK
    y_t = _ssm(u_t.reshape(groups, SSM_CHUNK * cpg, nc), mt, bct, cct, prm, seq // SSM_CHUNK, n_state)
    y_ssm = _glu(y_t.reshape(groups, SSM_CHUNK, cpg, nc), w_glu, rest, z_ssm_off)
    xc, xs = _chan_dft(rest, u_fft_off, fw, fw // w_fft.shape[0])
    y_fft = _seq_dft(xc, xs, w_fft, rest, z_fft_off, bsz, seq)
    merged = _merge(y_ssm, y_fft, w_up_ssm, w_up_fft, rest, gate_off)
    out = _matmul(merged, w_out, F32, "proj_out")
    return _post(out, x, post_g).reshape(bsz, seq, d)


def kernel(x_prompt, x_sample, pre_norm, post_norm, w_in, lambda_re, lambda_im, log_dt, b_re, b_im, c_re, c_im,
           d_skip, w_glu, w_fft, w_up_ssm, w_up_fft, w_out):
    depth = w_in.shape[0]
    d = x_prompt.shape[-1]
    sw = d_skip.shape[-1]
    fw = w_up_fft.shape[1]
    groups, n_state, cpg = b_re.shape[2], b_re.shape[3], b_re.shape[4]
    assert sw == groups * cpg and SSM_CHUNK * cpg == 4 * n_state
    assert w_in.shape[-1] == 2 * sw + 2 * fw + 2 * d
    dims = (sw, fw, groups, cpg, n_state)
    xs = [x_prompt, x_sample]
    for l in range(depth):
        w_ssm_t = w_in[l][:, :sw].T.astype(BF16)
        w_rest = w_in[l][:, sw:].astype(BF16)
        ssm_ops = _ssm_params(lambda_re[l], lambda_im[l], log_dt[l], b_re[l], b_im[l], c_re[l], c_im[l], d_skip[l])
        wl = (w_glu[l].astype(BF16), w_fft[l].astype(BF16), w_up_ssm[l].astype(BF16), w_up_fft[l].astype(BF16),
              w_out[l].astype(BF16))
        xs = [_layer(x, pre_norm[l], post_norm[l], (w_ssm_t, w_rest), ssm_ops, None, *wl, dims) for x in xs]
    return tuple(xs)
```

```python
import functools
import math

import numpy as np
import jax
import jax.numpy as jnp
from jax import lax
from jax.experimental import pallas as pl
from jax.experimental.pallas import tpu as pltpu

RMS_EPS = 1e-6
LAMBDA_RE_MAX = -1e-4
SSM_CHUNK = 16
N_SCAN_LANES = 8
VMEM_LIMIT_BYTES = 56 * 1024 * 1024
LANE = 128

F32 = jnp.float32
BF16 = jnp.bfloat16


def _params(*sem):
    return pltpu.CompilerParams(dimension_semantics=sem, vmem_limit_bytes=VMEM_LIMIT_BYTES)


def _tile(n, pref, *offsets, mult=LANE):
    if n <= pref and all(o % n == 0 for o in offsets):
        return n
    t = (min(pref, n) // mult) * mult
    while t >= mult:
        if n % t == 0 and all(o % t == 0 for o in offsets):
            return t
        t -= mult
    raise ValueError(f"no tile for {n} {pref} {offsets}")


def _sigmoid(x):
    return 1.0 / (1.0 + jnp.exp(-x))


def _rmsnorm_kernel(x_ref, g_ref, o_ref):
    for s in range(SSM_CHUNK):
        x = x_ref[:, s, :]
        ms = jnp.mean(x * x, axis=-1, keepdims=True)
        o_ref[s] = (x * lax.rsqrt(ms + RMS_EPS) * g_ref[...]).astype(o_ref.dtype)


def _rmsnorm(x, g, out_dtype):
    nc, tc, d = x.shape
    tm = _tile(nc, 32, mult=16)
    out = pl.pallas_call(
        _rmsnorm_kernel,
        grid=(nc // tm,),
        in_specs=[pl.BlockSpec((tm, tc, d), lambda i: (i, 0, 0)), pl.BlockSpec((1, d), lambda i: (0, 0))],
        out_specs=pl.BlockSpec((tc, tm, d), lambda i: (0, i, 0)),
        out_shape=jax.ShapeDtypeStruct((tc, nc, d), out_dtype),
        compiler_params=_params("parallel"),
        name="rmsnorm",
    )(x, g.reshape(1, d))
    return out.reshape(tc * nc, d)


def _mm_kernel(a_ref, b_ref, o_ref):
    o_ref[...] = jnp.dot(a_ref[...], b_ref[...], preferred_element_type=F32).astype(o_ref.dtype)


def _matmul(a, b, out_dtype, name, tm_pref=1024, tn_pref=1024):
    m, k = a.shape
    _, n = b.shape
    tm, tn = _tile(m, tm_pref), _tile(n, tn_pref)
    return pl.pallas_call(
        _mm_kernel,
        grid=(m // tm, n // tn),
        in_specs=[pl.BlockSpec((tm, k), lambda i, j: (i, 0)), pl.BlockSpec((k, tn), lambda i, j: (0, j))],
        out_specs=pl.BlockSpec((tm, tn), lambda i, j: (i, j)),
        out_shape=jax.ShapeDtypeStruct((m, n), out_dtype),
        compiler_params=_params("parallel", "parallel"),
        name=name,
    )(a, b)


def _proj_t_kernel(w_ref, h_ref, o_ref):
    r = lax.dot_general(w_ref[...], h_ref[...], (((1,), (1,)), ((), ())), preferred_element_type=F32)
    o_ref[...] = r.reshape(o_ref.shape)


def _proj_ssm_t(h, w_t, groups, cpg):
    t, d = h.shape
    sw = w_t.shape[0]
    nc = t // SSM_CHUNK
    tm, tn = _tile(nc, 512), _tile(sw, 1024)
    nbi = nc // tm
    return pl.pallas_call(
        _proj_t_kernel,
        grid=(nbi, SSM_CHUNK, sw // tn),
        in_specs=[pl.BlockSpec((tn, d), lambda i, s, j: (j, 0)),
                  pl.BlockSpec((tm, d), lambda i, s, j: (s * nbi + i, 0))],
        out_specs=pl.BlockSpec((tn // cpg, 1, cpg, tm), lambda i, s, j: (j, s, 0, i)),
        out_shape=jax.ShapeDtypeStruct((groups, SSM_CHUNK, cpg, nc), F32),
        compiler_params=_params("parallel", "parallel", "parallel"),
        name="proj_ssm_t",
    )(w_t, h)


def _ssm_kernel(u_ref, mt_ref, bct_ref, cct_ref, prm_ref, o_ref, *, chunks_per_seq, n_state):
    p = n_state
    v = u_ref[0]
    vb = v.astype(BF16)
    nc = v.shape[1]
    z = jnp.dot(bct_ref[0], vb, preferred_element_type=F32)
    fr, fi, br, bi = z[0:p], z[p:2 * p], z[2 * p:3 * p], z[3 * p:4 * p]
    pos = lax.broadcasted_iota(jnp.int32, (1, nc), 1) % chunks_per_seq
    prm = prm_ref[0]
    nsteps = int(math.log2(chunks_per_seq))
    for k in range(nsteps):
        d = 1 << k
        far, fai = prm[0:p, k:k + 1], prm[0:p, N_SCAN_LANES + k:N_SCAN_LANES + k + 1]
        bar = prm[0:p, 2 * N_SCAN_LANES + k:2 * N_SCAN_LANES + k + 1]
        bai = prm[0:p, 3 * N_SCAN_LANES + k:3 * N_SCAN_LANES + k + 1]
        mf = pos >= d
        sr = jnp.where(mf, pltpu.roll(fr, d, 1), 0.0)
        si = jnp.where(mf, pltpu.roll(fi, d, 1), 0.0)
        fr, fi = fr + far * sr - fai * si, fi + far * si + fai * sr
        mb = pos < chunks_per_seq - d
        sr = jnp.where(mb, pltpu.roll(br, nc - d, 1), 0.0)
        si = jnp.where(mb, pltpu.roll(bi, nc - d, 1), 0.0)
        br, bi = br + bar * sr - bai * si, bi + bar * si + bai * sr
    mf = pos >= 1
    mb = pos < chunks_per_seq - 1
    s_in = jnp.concatenate([
        jnp.where(mf, pltpu.roll(fr, 1, 1), 0.0), jnp.where(mf, pltpu.roll(fi, 1, 1), 0.0),
        jnp.where(mb, pltpu.roll(br, nc - 1, 1), 0.0), jnp.where(mb, pltpu.roll(bi, nc - 1, 1), 0.0),
    ], axis=0).astype(BF16)
    d_skip = prm[:, 4 * N_SCAN_LANES:4 * N_SCAN_LANES + 1]
    y = (jnp.dot(mt_ref[0], vb, preferred_element_type=F32)
         + jnp.dot(cct_ref[0], s_in, preferred_element_type=F32)
         + d_skip * v)
    o_ref[0] = jax.nn.gelu(y, approximate=True).astype(o_ref.dtype)


def _ssm(u, mt, bct, cct, prm, chunks_per_seq, n_state):
    g, r, nc = u.shape
    assert chunks_per_seq & (chunks_per_seq - 1) == 0 and chunks_per_seq <= (1 << N_SCAN_LANES)
    assert 4 * n_state == r
    blk3 = lambda a: pl.BlockSpec((1,) + a.shape[1:], lambda i: (i, 0, 0))
    return pl.pallas_call(
        functools.partial(_ssm_kernel, chunks_per_seq=chunks_per_seq, n_state=n_state),
        grid=(g,),
        in_specs=[blk3(u), blk3(mt), blk3(bct), blk3(cct), blk3(prm)],
        out_specs=pl.BlockSpec((1, r, nc), lambda i: (i, 0, 0)),
        out_shape=jax.ShapeDtypeStruct((g, r, nc), BF16),
        compiler_params=_params("parallel"),
        name="ssm_chunked",
    )(u, mt, bct, cct, prm)


def _ssm_ops_kernel(lamc_ref, lamr_ref, b4_ref, c4_ref, tile_ref, mt_ref, bct_ref, cct_ref, prm_ref, *, n_state):
    hi = lax.Precision.HIGHEST
    p, tc = n_state, SSM_CHUNK
    r = 4 * p
    cpg = r // tc
    shift = cpg.bit_length() - 1
    row = lax.broadcasted_iota(jnp.int32, (r, 1), 0)
    lane = lax.broadcasted_iota(jnp.int32, (1, r), 1)
    rb, lb = (row >> shift).astype(F32), (lane >> shift).astype(F32)
    row_fwd, lane_fwd = row < 2 * p, lane < 2 * p
    pshift = p.bit_length() - 1
    row_im = ((row >> pshift) & 1) == 1
    lane_im = ((lane >> pshift) & 1) == 1
    tile = tile_ref[...]

    def cexp(k, xr, xi):
        mag = jnp.exp(k * xr)
        return mag * jnp.cos(k * xi), mag * jnp.sin(k * xi)

    def roll_rows(a, s):
        return a if s % r == 0 else pltpu.roll(a, s % r, 0)

    def swap_quarters(a, axis):
        q = [lax.slice_in_dim(a, i * p, (i + 1) * p, axis=axis) for i in range(4)]
        return jnp.concatenate([q[1], q[0], q[3], q[2]], axis=axis)

    lamc = lamc_ref[0]
    lr = jnp.minimum(lamc[:, 0:1], LAMBDA_RE_MAX)
    li = lamc[:, 1:2]
    dt = jnp.exp(lamc[:, 2:3])
    xr_c, xi_c = lr * dt, li * dt
    ar, ai = cexp(1.0, xr_c, xi_c)
    den = lr * lr + li * li
    q_re = ((ar - 1.0) * lr + ai * li) / den
    q_im = (ai * lr - (ar - 1.0) * li) / den
    sgn_c = jnp.where(row_im, 1.0, -1.0)
    b4 = b4_ref[0]
    bb4 = q_re * b4 + sgn_c * q_im * swap_quarters(b4, 0)
    bb4t = jnp.dot(bb4, tile, precision=hi, preferred_element_type=F32)
    pwr, pwi = cexp(jnp.where(row_fwd, (tc - 1.0) - lb, lb), xr_c, xi_c)
    bct_ref[0] = (pwr * bb4t + sgn_c * pwi * swap_quarters(bb4t, 0)).astype(bct_ref.dtype)

    lamr = lamr_ref[0]
    lr_r = jnp.minimum(lamr[0:1], LAMBDA_RE_MAX)
    dt_r = jnp.exp(lamr[2:3])
    xr_r, xi_r = lr_r * dt_r, lamr[1:2] * dt_r
    c4 = c4_ref[0]
    c4t = jnp.concatenate([c4] * tc, axis=0)
    c4s = jnp.concatenate([swap_quarters(c4, 1)] * tc, axis=0)
    s_a = jnp.where(lane_im, -1.0, 1.0)

    def c_pow(k):
        pr, pi = cexp(k, xr_r, xi_r)
        return s_a * c4t * pr - c4s * pi

    cct_ref[0] = c_pow(jnp.where(lane_fwd, rb + 1.0, tc - rb)).astype(cct_ref.dtype)
    ee = c_pow(jnp.where(lane_fwd, rb, (tc - 1.0) - rb))
    kf = jnp.dot(ee[:, :2 * p], bb4[:2 * p], precision=hi, preferred_element_type=F32)
    kb = jnp.dot(ee[:, 2 * p:], bb4[2 * p:], precision=hi, preferred_element_type=F32)
    ktf = jnp.dot(kf, tile, precision=hi, preferred_element_type=F32)
    ktb = jnp.dot(kb, tile, precision=hi, preferred_element_type=F32)
    mt = jnp.zeros((r, r), F32)
    for t in range(tc):
        in_col = lb == float(t)
        fwd = roll_rows(ktf, cpg * t)
        bwd = roll_rows(ktb, r - cpg * (tc - 1 - t))
        mt = mt + jnp.where(in_col & (rb >= float(t)), fwd, 0.0) + jnp.where(in_col & (rb <= float(t)), bwd, 0.0)
    mt_ref[0] = mt.astype(mt_ref.dtype)

    pl_lane = lax.broadcasted_iota(jnp.int32, (1, LANE), 1)
    step = (tc * (1 << (pl_lane & (N_SCAN_LANES - 1)))).astype(F32)
    use_fwd = pl_lane < 2 * N_SCAN_LANES
    sx = jnp.where(use_fwd, xr_c[0:p], xr_c[2 * p:3 * p])
    sy = jnp.where(use_fwd, xi_c[0:p], xi_c[2 * p:3 * p])
    s_re, s_im = cexp(step, sx, sy)
    scan = jnp.where(((pl_lane >> (N_SCAN_LANES.bit_length() - 1)) & 1) == 0, s_re, s_im)
    prm = jnp.concatenate([scan, jnp.zeros((r - p, LANE), F32)], axis=0)
    prm_ref[0] = jnp.where(pl_lane == 4 * N_SCAN_LANES, lamc[:, 3:4], prm)


def _ssm_operators(lambda_re, lambda_im, log_dt, b_re, b_im, c_re, c_im, d_skip):
    ndir, g, p = lambda_re.shape
    cpg = b_re.shape[-1]
    tc = SSM_CHUNK
    r = tc * cpg
    assert ndir == 2 and r == 4 * p and cpg <= LANE and cpg & (cpg - 1) == 0 and p & (p - 1) == 0
    f32 = lambda a: a.astype(F32)
    quarters = lambda a: jnp.concatenate([a[0], a[0], a[1], a[1]], axis=-1)
    lam3 = jnp.stack([quarters(f32(lambda_re)), quarters(f32(lambda_im)),
                      quarters(jnp.broadcast_to(f32(log_dt)[..., None], (2, g, p)))], axis=1)
    dsk = jnp.tile(f32(d_skip).reshape(g, 1, cpg), (1, tc, 1)).reshape(g, 1, r)
    lamc = jnp.concatenate([lam3, dsk], axis=1).transpose(0, 2, 1)
    lamr = jnp.concatenate([lam3, jnp.zeros((g, 5, r), F32)], axis=1)
    b4 = jnp.concatenate([f32(b_re[0]), f32(b_im[0]), f32(b_re[1]), f32(b_im[1])], axis=1)
    b4 = jnp.pad(b4, ((0, 0), (0, 0), (0, LANE - cpg)))
    c4 = jnp.concatenate([f32(c_re[0]), f32(c_im[0]), f32(c_re[1]), f32(c_im[1])], axis=2)
    tile = jnp.asarray(np.arange(LANE)[:, None] == (np.arange(r)[None, :] % cpg), F32)
    blk = lambda a: pl.BlockSpec((1,) + a.shape[1:], lambda i: (i, 0, 0))
    op = jax.ShapeDtypeStruct((g, r, r), BF16)
    return pl.pallas_call(
        functools.partial(_ssm_ops_kernel, n_state=p),
        grid=(g,),
        in_specs=[blk(lamc), blk(lamr), blk(b4), blk(c4), pl.BlockSpec(tile.shape, lambda i: (0, 0))],
        out_specs=[pl.BlockSpec((1, r, r), lambda i: (i, 0, 0))] * 3 + [pl.BlockSpec((1, r, LANE), lambda i: (i, 0, 0))],
        out_shape=[op, op, op, jax.ShapeDtypeStruct((g, r, LANE), F32)],
        compiler_params=_params("parallel"),
        name="ssm_operators",
    )(lamc, lamr, b4, c4, tile)


def _glu_kernel(yt_ref, wv_ref, wg_ref, z_ref, o_ref):
    a = yt_ref[...].reshape(wv_ref.shape[0], yt_ref.shape[-1])
    dn = (((0,), (0,)), ((), ()))
    val = lax.dot_general(a, wv_ref[...], dn, preferred_element_type=F32)
    gate = lax.dot_general(a, wg_ref[...], dn, preferred_element_type=F32)
    z = z_ref[...].astype(F32)
    o_ref[...] = (val * _sigmoid(gate) * (z * _sigmoid(z))).astype(o_ref.dtype)


def _glu(yt, w_glu, rest, z_off):
    g, tc, cpg, nc = yt.shape
    sw = g * cpg
    tm, tn = _tile(nc, 512), _tile(sw, 1024, z_off)
    nb, nbi = sw // tn, nc // tm
    return pl.pallas_call(
        _glu_kernel,
        grid=(nbi, tc, nb),
        in_specs=[pl.BlockSpec((g, 1, cpg, tm), lambda i, s, j: (0, s, 0, i)),
                  pl.BlockSpec((sw, tn), lambda i, s, j: (0, j)),
                  pl.BlockSpec((sw, tn), lambda i, s, j: (0, nb + j)),
                  pl.BlockSpec((tm, tn), lambda i, s, j: (s * nbi + i, z_off // tn + j))],
        out_specs=pl.BlockSpec((tm, tn), lambda i, s, j: (s * nbi + i, j)),
        out_shape=jax.ShapeDtypeStruct((nc * tc, sw), BF16),
        compiler_params=_params("parallel", "parallel", "parallel"),
        name="glu_gate",
    )(yt, w_glu, w_glu, rest)


def _dft_tables(n, scale):
    ncs = n // SSM_CHUNK
    idx = np.arange(n, dtype=np.int64)
    k = ((idx % ncs) * SSM_CHUNK + idx // ncs)[:, None]
    hi_ang = 2.0 * np.pi * ((k * (np.arange(ncs, dtype=np.int64)[None, :] * SSM_CHUNK)) % n) / n
    lo_ang = 2.0 * np.pi * ((k * np.arange(SSM_CHUNK, dtype=np.int64)[None, :]) % n) / n
    c1, s1 = jnp.asarray(np.cos(hi_ang), F32)[:, None, :], jnp.asarray(np.sin(hi_ang), F32)[:, None, :]
    c2, s2 = jnp.asarray(np.cos(lo_ang) * scale, F32)[:, :, None], jnp.asarray(np.sin(lo_ang) * scale, F32)[:, :, None]
    cos = (c1 * c2 - s1 * s2).reshape(n, n).astype(BF16)
    nsin = (-(s1 * c2 + c1 * s2)).reshape(n, n).astype(BF16)
    return cos, nsin


def _chan_dft_kernel(u_ref, cs_ref, xc_ref, xs_ref):
    r = jnp.dot(u_ref[...], cs_ref[...], preferred_element_type=F32)
    w = xc_ref.shape[1]
    xc_ref[...] = r[:, :w].astype(xc_ref.dtype)
    xs_ref[...] = r[:, w:].astype(xs_ref.dtype)


def _chan_dft(rest, u_off, fw, cg):
    t = rest.shape[0]
    k = np.arange(cg, dtype=np.int64)
    ang = 2.0 * np.pi * ((k[:, None] * k[None, :]) % cg) / cg
    cs = jnp.asarray(np.concatenate([np.cos(ang), np.sin(ang)], axis=1) / math.sqrt(cg), F32).astype(BF16)
    tm = _tile(t, 1024)
    assert u_off % cg == 0
    spec = pl.BlockSpec((tm, cg), lambda i, j: (i, j))
    return pl.pallas_call(
        _chan_dft_kernel,
        grid=(t // tm, fw // cg),
        in_specs=[pl.BlockSpec((tm, cg), lambda i, j: (i, u_off // cg + j)),
                  pl.BlockSpec((cg, 2 * cg), lambda i, j: (0, 0))],
        out_specs=[spec, spec],
        out_shape=[jax.ShapeDtypeStruct((t, fw), BF16)] * 2,
        compiler_params=_params("parallel", "parallel"),
        name="chan_dft",
    )(rest, cs)


def _seq_dft_kernel(cos_ref, nsin_ref, xc_ref, xs_ref, w_ref, z_ref, o_ref):
    seq, cg = cos_ref.shape[1], xc_ref.shape[-1]
    f = (jnp.dot(cos_ref[...], xc_ref[...].reshape(seq, cg), preferred_element_type=F32)
         + jnp.dot(nsin_ref[...], xs_ref[...].reshape(seq, cg), preferred_element_type=F32))
    y = jnp.dot(f.astype(BF16), w_ref[0], preferred_element_type=F32)
    z = z_ref[...].reshape(y.shape).astype(F32)
    o_ref[...] = (y * (z * _sigmoid(z))).astype(o_ref.dtype).reshape(o_ref.shape)


def _seq_dft(xc, xs, w_fft, rest, z_off, bsz, seq):
    t, fw = xc.shape
    ng, cg, _ = w_fft.shape
    nr = rest.shape[1]
    tc, ncs = SSM_CHUNK, seq // SSM_CHUNK
    cos, nsin = _dft_tables(seq, 1.0 / math.sqrt(seq))
    tk = _tile(seq, 512)
    tks = tk // ncs
    assert z_off % cg == 0 and tk % ncs == 0
    view = lambda a: a.reshape(tc, bsz, ncs, a.shape[-1])
    seq_blk = pl.BlockSpec((tc, 1, ncs, cg), lambda b, j, k: (0, b, 0, j))
    out = pl.pallas_call(
        _seq_dft_kernel,
        grid=(bsz, ng, seq // tk),
        in_specs=[pl.BlockSpec((tk, seq), lambda b, j, k: (k, 0)),
                  pl.BlockSpec((tk, seq), lambda b, j, k: (k, 0)),
                  seq_blk, seq_blk,
                  pl.BlockSpec((1, cg, cg), lambda b, j, k: (j, 0, 0)),
                  pl.BlockSpec((tks, 1, ncs, cg), lambda b, j, k: (k, b, 0, z_off // cg + j))],
        out_specs=pl.BlockSpec((tks, 1, ncs, cg), lambda b, j, k: (k, b, 0, j)),
        out_shape=jax.ShapeDtypeStruct((tc, bsz, ncs, fw), BF16),
        compiler_params=_params("parallel", "parallel", "parallel"),
        name="seq_dft",
    )(cos, nsin, view(xc), view(xs), w_fft, view(rest))
    return out.reshape(t, fw)


def _merge_kernel(ys_ref, yf_ref, ws_ref, wf_ref, g0_ref, g1_ref, o_ref):
    ps = jnp.dot(ys_ref[...], ws_ref[...], preferred_element_type=F32)
    pf = jnp.dot(yf_ref[...], wf_ref[...], preferred_element_type=F32)
    g0 = _sigmoid(g0_ref[...].astype(F32))
    g1 = _sigmoid(g1_ref[...].astype(F32))
    o_ref[...] = (g0 * ps + g1 * pf).astype(o_ref.dtype)


def _merge(ys, yf, w_up_ssm, w_up_fft, rest, g_off):
    t, sw = ys.shape
    fw = yf.shape[1]
    d = w_up_ssm.shape[1]
    tm, tn = _tile(t, 1024), _tile(d, 1024, g_off)
    return pl.pallas_call(
        _merge_kernel,
        grid=(t // tm, d // tn),
        in_specs=[pl.BlockSpec((tm, sw), lambda i, j: (i, 0)),
                  pl.BlockSpec((tm, fw), lambda i, j: (i, 0)),
                  pl.BlockSpec((sw, tn), lambda i, j: (0, j)),
                  pl.BlockSpec((fw, tn), lambda i, j: (0, j)),
                  pl.BlockSpec((tm, tn), lambda i, j: (i, g_off // tn + j)),
                  pl.BlockSpec((tm, tn), lambda i, j: (i, (g_off + d) // tn + j))],
        out_specs=pl.BlockSpec((tm, tn), lambda i, j: (i, j)),
        out_shape=jax.ShapeDtypeStruct((t, d), BF16),
        compiler_params=_params("parallel", "parallel"),
        name="merge",
    )(ys, yf, w_up_ssm, w_up_fft, rest, rest)


def _post_kernel(o_ref, x_ref, g_ref, y_ref):
    for s in range(SSM_CHUNK):
        o = o_ref[s]
        ms = jnp.mean(o * o, axis=-1, keepdims=True)
        y_ref[:, s, :] = x_ref[:, s, :] + o * lax.rsqrt(ms + RMS_EPS) * g_ref[...]


def _post(o, x, g):
    nc, tc, d = x.shape
    tm = _tile(nc, 16, mult=8)
    xspec = pl.BlockSpec((tm, tc, d), lambda i: (i, 0, 0))
    return pl.pallas_call(
        _post_kernel,
        grid=(nc // tm,),
        in_specs=[pl.BlockSpec((tc, tm, d), lambda i: (0, i, 0)), xspec, pl.BlockSpec((1, d), lambda i: (0, 0))],
        out_specs=xspec,
        out_shape=jax.ShapeDtypeStruct((nc, tc, d), F32),
        compiler_params=_params("parallel"),
        name="post_norm",
    )(o.reshape(tc, nc, d), x, g.reshape(1, d))


def _layer(x3, pre_g, post_g, w_in, ssm_ops, w_glu, w_fft, w_up_ssm, w_up_fft, w_out, dims):
    bsz, seq, d = x3.shape
    sw, fw, groups, cpg, n_state = dims
    x = x3.reshape(bsz * seq // SSM_CHUNK, SSM_CHUNK, d)
    w_ssm_t, w_rest = w_in
    h = _rmsnorm(x, pre_g, BF16)
    u_t = _proj_ssm_t(h, w_ssm_t, groups, cpg)
    rest = _matmul(h, w_rest, BF16, "proj_rest")
    z_ssm_off, u_fft_off, z_fft_off, gate_off = 0, sw, sw + fw, sw + 2 * fw
    mt, bct, cct, prm = ssm_ops
    nc = bsz * seq // SSM_CHUNK
    y_t = _ssm(u_t.reshape(groups, SSM_CHUNK * cpg, nc), mt, bct, cct, prm, seq // SSM_CHUNK, n_state)
    y_ssm = _glu(y_t.reshape(groups, SSM_CHUNK, cpg, nc), w_glu, rest, z_ssm_off)
    xc, xs = _chan_dft(rest, u_fft_off, fw, fw // w_fft.shape[0])
    y_fft = _seq_dft(xc, xs, w_fft, rest, z_fft_off, bsz, seq)
    merged = _merge(y_ssm, y_fft, w_up_ssm, w_up_fft, rest, gate_off)
    out = _matmul(merged, w_out, F32, "proj_out")
    return _post(out, x, post_g).reshape(bsz, seq, d)


def kernel(x_prompt, x_sample, pre_norm, post_norm, w_in, lambda_re, lambda_im, log_dt, b_re, b_im, c_re, c_im,
           d_skip, w_glu, w_fft, w_up_ssm, w_up_fft, w_out):
    depth = w_in.shape[0]
    d = x_prompt.shape[-1]
    sw = d_skip.shape[-1]
    fw = w_up_fft.shape[1]
    groups, n_state, cpg = b_re.shape[2], b_re.shape[3], b_re.shape[4]
    assert sw == groups * cpg and SSM_CHUNK * cpg == 4 * n_state
    assert w_in.shape[-1] == 2 * sw + 2 * fw + 2 * d
    dims = (sw, fw, groups, cpg, n_state)
    xs = [x_prompt, x_sample]
    for l in range(depth):
        w_ssm_t = w_in[l][:, :sw].T.astype(BF16)
        w_rest = w_in[l][:, sw:].astype(BF16)
        ssm_ops = _ssm_operators(lambda_re[l], lambda_im[l], log_dt[l], b_re[l], b_im[l], c_re[l], c_im[l], d_skip[l])
        wl = (w_glu[l].astype(BF16), w_fft[l].astype(BF16), w_up_ssm[l].astype(BF16), w_up_fft[l].astype(BF16),
              w_out[l].astype(BF16))
        xs = [_layer(x, pre_norm[l], post_norm[l], (w_ssm_t, w_rest), ssm_ops, *wl, dims) for x in xs]
    return tuple(xs)
```
